```python
import jax, jax.numpy as jnp
from jax import lax
import numpy as np

D_MODEL = 1024
BATCH = 8
SEQ = 4096
DEPTH = 1

HEAD_DIM = 64
N_HEADS = 8
N_KV_HEADS = 2
GROUP = N_HEADS // N_KV_HEADS
WINDOW = 128
BLOCK = 128
ATTN_SCALE = HEAD_DIM ** -0.5
ATTN_WIDTH = N_HEADS * HEAD_DIM
KV_WIDTH = N_KV_HEADS * HEAD_DIM
CONV_GROUPS = 8
CONV_WIDTH = CONV_GROUPS * 64
CONV_K = 3
IN_WIDTH = ATTN_WIDTH + 2 * KV_WIDTH + 3 * CONV_WIDTH + 2 * D_MODEL
D_FF = 2816
FFN_CONV_K = 3
NORM_EPS = 1e-5

kernel_name = "hybrid_swa_sink_shortconv_gated_convffn"


def rms_norm(x, g):
    xf = x.astype(jnp.float32)
    y = xf * lax.rsqrt(jnp.mean(xf * xf, axis=-1, keepdims=True) + NORM_EPS)
    return (y * g.astype(jnp.float32)).astype(x.dtype)


def causal_depthwise_conv(x, w):
    k_width, ch = w.shape
    return lax.conv_general_dilated(
        x, w[:, None, :].astype(x.dtype), window_strides=(1,), padding=((k_width - 1, 0),),
        dimension_numbers=("NWC", "WIO", "NWC"), feature_group_count=ch)


def sliding_window_attention(q, k, v, sinks):
    b, s, _ = q.shape
    nb = s // BLOCK
    q = q.reshape(b, nb, BLOCK, N_KV_HEADS, GROUP, HEAD_DIM)
    k = k.reshape(b, nb, BLOCK, N_KV_HEADS, HEAD_DIM)
    v = v.reshape(b, nb, BLOCK, N_KV_HEADS, HEAD_DIM)

    def with_prev(t):
        prev = jnp.pad(t, ((0, 0), (1, 0), (0, 0), (0, 0), (0, 0)))[:, :-1]
        return jnp.concatenate([prev, t], axis=2)

    kw, vw = with_prev(k), with_prev(v)
    scores = jnp.einsum("bnqhgd,bnkhd->bnhgqk", q, kw).astype(jnp.float32) * ATTN_SCALE
    qi = jnp.arange(BLOCK)[:, None]
    kj = jnp.arange(2 * BLOCK)[None, :]
    dist = qi + BLOCK - kj
    band = (dist >= 0) & (dist < WINDOW)
    real = (jnp.arange(nb)[:, None, None] > 0) | (kj[None] >= BLOCK)
    mask = band[None] & real
    scores = jnp.where(mask[None, :, None, None], scores, -jnp.inf)
    sink = jnp.broadcast_to(sinks.astype(jnp.float32).reshape(1, 1, N_KV_HEADS, GROUP, 1, 1),
                            scores.shape[:-1] + (1,))
    probs = jax.nn.softmax(jnp.concatenate([scores, sink], axis=-1), axis=-1)[..., :-1]
    out = jnp.einsum("bnhgqk,bnkhd->bnqhgd", probs.astype(v.dtype), vw)
    return out.reshape(b, s, ATTN_WIDTH)


def setup_inputs(seed: int = 0) -> dict:
    key = jax.random.key(seed)
    ks = jax.random.split(key, 16)
    f32 = jnp.float32

    def nrm(k, shape, scale):
        return jax.random.normal(k, shape, f32) * scale

    return {
        "x": nrm(ks[0], (BATCH, SEQ, D_MODEL), 1.0),
        "mix_norm": 1.0 + nrm(ks[1], (DEPTH, D_MODEL), 0.02),
        "w_in": nrm(ks[2], (DEPTH, D_MODEL, IN_WIDTH), D_MODEL ** -0.5),
        "b_in": nrm(ks[3], (DEPTH, IN_WIDTH), 0.02),
        "sinks": nrm(ks[4], (DEPTH, N_HEADS), 0.5),
        "conv_w": nrm(ks[5], (DEPTH, CONV_K, CONV_WIDTH), CONV_K ** -0.5),
        "w_attn_branch": nrm(ks[6], (DEPTH, ATTN_WIDTH, D_MODEL), ATTN_WIDTH ** -0.5),
        "w_conv_branch": nrm(ks[7], (DEPTH, CONV_WIDTH, D_MODEL), CONV_WIDTH ** -0.5),
        "w_out": nrm(ks[8], (DEPTH, D_MODEL, D_MODEL), D_MODEL ** -0.5),
        "ffn_norm": 1.0 + nrm(ks[9], (DEPTH, D_MODEL), 0.02),
        "w_up": nrm(ks[10], (DEPTH, D_MODEL, 2 * D_FF), D_MODEL ** -0.5),
        "ffn_conv_w": nrm(ks[11], (DEPTH, FFN_CONV_K, 2 * D_FF), FFN_CONV_K ** -0.5),
        "w_down": nrm(ks[12], (DEPTH, D_FF, D_MODEL), D_FF ** -0.5),
        "final_norm": 1.0 + nrm(ks[13], (D_MODEL,), 0.02),
    }


def reference(x, mix_norm, w_in, b_in, sinks, conv_w, w_attn_branch, w_conv_branch, w_out,
              ffn_norm, w_up, ffn_conv_w, w_down, final_norm):
    h = x
    splits = np.cumsum([ATTN_WIDTH, KV_WIDTH, KV_WIDTH, CONV_WIDTH, CONV_WIDTH, CONV_WIDTH, D_MODEL])
    for l in range(DEPTH):
        xn = rms_norm(h, mix_norm[l])
        proj = jnp.einsum("bsd,dp->bsp", xn, w_in[l]) + b_in[l]
        q, k, v, cb, cc, cx, ga, gc = jnp.split(proj, splits, axis=-1)
        attn = sliding_window_attention(q, k, v, sinks[l])
        conv = cb * causal_depthwise_conv(cc * cx, conv_w[l])
        merged = (jax.nn.sigmoid(ga) * jnp.einsum("bsc,cd->bsd", attn, w_attn_branch[l])
                  + jax.nn.sigmoid(gc) * jnp.einsum("bsc,cd->bsd", conv, w_conv_branch[l]))
        h = h + jnp.einsum("bsd,de->bse", merged, w_out[l])
        hn = rms_norm(h, ffn_norm[l])
        up = causal_depthwise_conv(jnp.einsum("bsd,df->bsf", hn, w_up[l]), ffn_conv_w[l])
        gate, val = jnp.split(up, 2, axis=-1)
        h = h + jnp.einsum("bsf,fd->bsd", jax.nn.silu(gate) * val, w_down[l])
    return rms_norm(h, final_norm)
```

```python
import functools

import jax
import jax.numpy as jnp
from jax import lax
from jax.experimental import pallas as pl
from jax.experimental.pallas import tpu as pltpu

D_MODEL = 1024
HEAD_DIM = 64
N_HEADS = 8
N_KV_HEADS = 2
GROUP = N_HEADS // N_KV_HEADS
WINDOW = 128
BLOCK = 128
ATTN_SCALE = HEAD_DIM ** -0.5
ATTN_WIDTH = N_HEADS * HEAD_DIM
KV_WIDTH = N_KV_HEADS * HEAD_DIM
CONV_WIDTH = 512
CONV_K = 3
IN_WIDTH = ATTN_WIDTH + 2 * KV_WIDTH + 3 * CONV_WIDTH + 2 * D_MODEL
D_FF = 2816
NORM_EPS = 1e-5

Q_OFF = 0
K_OFF = ATTN_WIDTH
V_OFF = K_OFF + KV_WIDTH
CB_OFF = V_OFF + KV_WIDTH
CC_OFF = CB_OFF + CONV_WIDTH
CX_OFF = CC_OFF + CONV_WIDTH
GA_OFF = CX_OFF + CONV_WIDTH
GC_OFF = GA_OFF + D_MODEL

LANES = 128
SUBLANES = 8
HALO = SUBLANES
TILE = 512
FFN_CHUNK = 256
VMEM_LIMIT_BYTES = 58 * 1024 * 1024

F32 = jnp.float32
BF16 = jnp.bfloat16


def _rms(x, g):
    ms = jnp.mean(x * x, axis=-1, keepdims=True)
    return x * lax.rsqrt(ms + NORM_EPS) * g


def _dot(a, b):
    return jnp.dot(a, b, preferred_element_type=F32)


def _dot_nt(a, b):
    return lax.dot_general(a, b, (((1,), (1,)), ((), ())), preferred_element_type=F32)


def _layer_kernel(sinks_ref, x_ref, mixg_ref, w_in_ref, b_in_ref, convw_ref, w_attn_ref, w_conv_ref, w_out_ref,
                  ffng_ref, w_up_ref, ffncw_ref, w_down_ref, fing_ref, o_ref,
                  q_buf, k_ext, v_ext, attn_buf, u_buf, up_buf, ffn_halo, act_buf):
    s = pl.program_id(1)
    T = TILE

    @pl.when(s == 0)
    def _():
        k_ext[:, 0:BLOCK, :] = jnp.zeros((4, BLOCK, LANES), BF16)
        v_ext[:, 0:BLOCK, :] = jnp.zeros((4, BLOCK, 2 * LANES), BF16)
        u_buf[0:HALO, :] = jnp.zeros((HALO, CONV_WIDTH), F32)
        ffn_halo[...] = jnp.zeros((HALO, 2 * D_FF), F32)

    @pl.when(s > 0)
    def _():
        for e in range(4):
            k_ext[e, 0:BLOCK, :] = k_ext[e, T:T + BLOCK, :]
            v_ext[e, 0:BLOCK, :] = v_ext[e, T:T + BLOCK, :]

    x = x_ref[...]
    xn =_rms(x, mixg_ref[...]).astype(BF16)

    def proj(lo, width):
        return _dot(xn, w_in_ref[:, lo:lo + width]) + b_in_ref[:, lo:lo + width]

    q_buf[...] = (proj(Q_OFF, ATTN_WIDTH) * ATTN_SCALE).astype(BF16)
    k = proj(K_OFF, KV_WIDTH)
    v = proj(V_OFF, KV_WIDTH)
    lane = lax.broadcasted_iota(jnp.int32, (T, LANES), 1)
    lo_half = lane < HEAD_DIM
    k_sw = pltpu.roll(k, HEAD_DIM, axis=1)
    v_sw = pltpu.roll(v, HEAD_DIM, axis=1)
    zero = jnp.zeros((T, LANES), F32)
    one = jnp.ones((T, LANES), F32)
    ones_lo = jnp.where(lo_half, one, zero).astype(BF16)
    ones_hi = jnp.where(lo_half, zero, one).astype(BF16)
    rows = slice(BLOCK, BLOCK + T)
    for h, (t_lo, t_hi) in enumerate(((k, k_sw), (k_sw, k))):
        k_ext[2 * h, rows, :] = jnp.where(lo_half, t_lo, zero).astype(BF16)
        k_ext[2 * h + 1, rows, :] = jnp.where(lo_half, zero, t_hi).astype(BF16)
    for h, (t_lo, t_hi) in enumerate(((v, v_sw), (v_sw, v))):
        v_ext[2 * h, rows, 0:LANES] = jnp.where(lo_half, t_lo, zero).astype(BF16)
        v_ext[2 * h, rows, LANES:] = ones_lo
        v_ext[2 * h + 1, rows, 0:LANES] = jnp.where(lo_half, zero, t_hi).astype(BF16)
        v_ext[2 * h + 1, rows, LANES:] = ones_hi

    qi = lax.broadcasted_iota(jnp.int32, (2 * BLOCK, 2 * BLOCK), 0) % BLOCK
    kj = lax.broadcasted_iota(jnp.int32, (2 * BLOCK, 2 * BLOCK), 1)
    dist = qi + BLOCK - kj
    band = (dist >= 0) & (dist < WINDOW)
    first_key = jnp.where(s == 0, BLOCK, 0)
    band_first = band & (kj >= first_key)
    top_rows = lax.broadcasted_iota(jnp.int32, (2 * BLOCK, 1), 0) < BLOCK
    lo_half_o = lax.broadcasted_iota(jnp.int32, (2 * BLOCK, LANES), 1) < HEAD_DIM

    for j in range(T // BLOCK):
        mask = band_first if j == 0 else band
        qrows = slice(j * BLOCK, (j + 1) * BLOCK)
        krows = slice(j * BLOCK, (j + 2) * BLOCK)
        for h in range(N_KV_HEADS):
            qq = jnp.concatenate([q_buf[qrows, (2 * h) * LANES:(2 * h + 1) * LANES],
                                  q_buf[qrows, (2 * h + 1) * LANES:(2 * h + 2) * LANES]], axis=0)
            acc = None
            esink = []
            for half in range(2):
                e = 2 * h + half
                sc = _dot_nt(qq, k_ext[e, krows, :])
                sc = jnp.where(mask, sc, -jnp.inf)
                sink = jnp.where(top_rows, sinks_ref[GROUP * h + half], sinks_ref[GROUP * h + 2 + half])
                m = jnp.maximum(jnp.max(sc, axis=-1, keepdims=True), sink)
                p = jnp.exp(sc - m).astype(BF16)
                pv = _dot(p, v_ext[e, krows, :])
                acc = pv if acc is None else acc + pv
                esink.append(jnp.exp(sink - m))
            den = acc[:, LANES:] + jnp.where(lo_half_o, esink[0], esink[1])
            o = (acc[:, :LANES] / den).astype(BF16)
            attn_buf[qrows, (2 * h) * LANES:(2 * h + 1) * LANES] = o[:BLOCK]
            attn_buf[qrows, (2 * h + 1) * LANES:(2 * h + 2) * LANES] = o[BLOCK:]


    u = proj(CC_OFF, CONV_WIDTH) * proj(CX_OFF, CONV_WIDTH)
    u_buf[HALO:HALO + T, :] = u
    conv = (convw_ref[0:1, :] * u_buf[HALO - 2:HALO - 2 + T, :]
            + convw_ref[1:2, :] * u_buf[HALO - 1:HALO - 1 + T, :]
            + convw_ref[2:3, :] * u)
    conv = (proj(CB_OFF, CONV_WIDTH) * conv).astype(BF16)
    u_buf[0:HALO, :] = u_buf[T:T + HALO, :]

    merged = (jax.nn.sigmoid(proj(GA_OFF, D_MODEL)) * _dot(attn_buf[...], w_attn_ref[...])
              + jax.nn.sigmoid(proj(GC_OFF, D_MODEL)) * _dot(conv, w_conv_ref[...]))
    hmid = x + _dot(merged.astype(BF16), w_out_ref[...])
    o_ref[...] = hmid
    hn = _rms(hmid, ffng_ref[...]).astype(BF16)

    for c in range(D_FF // FFN_CHUNK):
        buf = up_buf.at[c % 2]
        for part, off in enumerate((c * FFN_CHUNK, D_FF + c * FFN_CHUNK)):
            cols = slice(off, off + FFN_CHUNK)
            bcols = slice(part * FFN_CHUNK, (part + 1) * FFN_CHUNK)
            buf[0:HALO, bcols] = ffn_halo[:, cols]
            buf[HALO:HALO + T, bcols] = _dot(hn, w_up_ref[:, cols])
            ffn_halo[:, cols] = buf[T:T + HALO, bcols]
        conved = []
        for part, off in enumerate((c * FFN_CHUNK, D_FF + c * FFN_CHUNK)):
            cols = slice(off, off + FFN_CHUNK)
            bcols = slice(part * FFN_CHUNK, (part + 1) * FFN_CHUNK)
            conved.append(ffncw_ref[0:1, cols] * buf[HALO - 2:HALO - 2 + T, bcols]
                          + ffncw_ref[1:2, cols] * buf[HALO - 1:HALO - 1 + T, bcols]
                          + ffncw_ref[2:3, cols] * buf[HALO:HALO + T, bcols])
        gate, val = conved
        act_buf[:, c * FFN_CHUNK:(c + 1) * FFN_CHUNK] = (jax.nn.silu(gate) * val).astype(BF16)

    hout = o_ref[...] + _dot(act_buf[...], w_down_ref[...])
    o_ref[...] = _rms(hout, fing_ref[...])


def _resident(shape):
    return pl.BlockSpec(shape, lambda b, s: (0,) * len(shape), pipeline_mode=pl.Buffered(1))


@jax.jit
def kernel(x, mix_norm, w_in, b_in, sinks, conv_w, w_attn_branch, w_conv_branch, w_out, ffn_norm, w_up, ffn_conv_w,
           w_down, final_norm):
    batch, seq, d_model = x.shape
    assert d_model == D_MODEL and seq % TILE == 0 and mix_norm.shape[0] == 1
    T = TILE
    tile_spec = pl.BlockSpec((None, T, D_MODEL), lambda b, s: (b, s, 0))
    in_specs = [
        pl.BlockSpec(memory_space=pltpu.SMEM),
        tile_spec,
        _resident((1, D_MODEL)),
        _resident((D_MODEL, IN_WIDTH)),
        _resident((1, IN_WIDTH)),
        _resident((CONV_K, CONV_WIDTH)),
        _resident((ATTN_WIDTH, D_MODEL)),
        _resident((CONV_WIDTH, D_MODEL)),
        _resident((D_MODEL, D_MODEL)),
        _resident((1, D_MODEL)),
        _resident((D_MODEL, 2 * D_FF)),
        _resident((CONV_K, 2 * D_FF)),
        _resident((D_FF, D_MODEL)),
        _resident((1, D_MODEL)),
    ]
    scratch = [
        pltpu.VMEM((T, ATTN_WIDTH), BF16),
        pltpu.VMEM((4, T + BLOCK, LANES), BF16),
        pltpu.VMEM((4, T + BLOCK, 2 * LANES), BF16),
        pltpu.VMEM((T, ATTN_WIDTH), BF16),
        pltpu.VMEM((T + HALO, CONV_WIDTH), F32),
        pltpu.VMEM((2, T + HALO, 2 * FFN_CHUNK), F32),
        pltpu.VMEM((HALO, 2 * D_FF), F32),
        pltpu.VMEM((T, D_FF), BF16),
    ]
    call = pl.pallas_call(
        _layer_kernel,
        grid=(batch, seq // T),
        in_specs=in_specs,
        out_specs=tile_spec,
        out_shape=jax.ShapeDtypeStruct(x.shape, F32),
        scratch_shapes=scratch,
        compiler_params=pltpu.CompilerParams(
            dimension_semantics=("arbitrary", "arbitrary"),
            vmem_limit_bytes=VMEM_LIMIT_BYTES),
        name="hybrid_layer",
    )
    return call(
        sinks[0], x, mix_norm, w_in[0].astype(BF16), b_in, conv_w[0],
        w_attn_branch[0].astype(BF16), w_conv_branch[0].astype(BF16), w_out[0].astype(BF16),
        ffn_norm, w_up[0].astype(BF16), ffn_conv_w[0], w_down[0].astype(BF16), final_norm.reshape(1, D_MODEL))
```

```python
import jax
import jax.numpy as jnp
from jax import lax
from jax.experimental import pallas as pl
from jax.experimental.pallas import tpu as pltpu

D_MODEL = 1024
HEAD_DIM = 64
N_HEADS = 8
N_KV_HEADS = 2
GROUP = N_HEADS // N_KV_HEADS
WINDOW = 128
BLOCK = 128
ATTN_SCALE = HEAD_DIM ** -0.5
ATTN_WIDTH = N_HEADS * HEAD_DIM
KV_WIDTH = N_KV_HEADS * HEAD_DIM
CONV_WIDTH = 512
CONV_K = 3
IN_WIDTH = ATTN_WIDTH + 2 * KV_WIDTH + 3 * CONV_WIDTH + 2 * D_MODEL
D_FF = 2816
NORM_EPS = 1e-5

Q_OFF = 0
K_OFF = ATTN_WIDTH
V_OFF = K_OFF + KV_WIDTH
CB_OFF = V_OFF + KV_WIDTH
CC_OFF = CB_OFF + CONV_WIDTH
CX_OFF = CC_OFF + CONV_WIDTH
GA_OFF = CX_OFF + CONV_WIDTH
GC_OFF = GA_OFF + D_MODEL

LANES = 128
SUBLANES = 8
HALO = SUBLANES
TILE = 512
RUN = TILE // SUBLANES
HEAD = (3 - 1) * SUBLANES
FFN_CHUNK = 256
VMEM_LIMIT_BYTES = 58 * 1024 * 1024

F32 = jnp.float32
BF16 = jnp.bfloat16


def _rms(x, g):
    ms = jnp.mean(x * x, axis=-1, keepdims=True)
    return x * lax.rsqrt(ms + NORM_EPS) * g


def _dot(a, b):
    return jnp.dot(a, b, preferred_element_type=F32)


def _dot_nt(a, b):
    return lax.dot_general(a, b, (((1,), (1,)), ((), ())), preferred_element_type=F32)


def _layer_kernel(sinks_ref, x_ref, mixg_ref, w_in_ref, b_in_ref, convw_ref, w_attn_ref, w_conv_ref, w_out_ref,
                  ffng_ref, w_up_ref, ffncw_ref, w_down_ref, fing_ref, o_ref,
                  q_buf, k_ext, v_ext, attn_buf, u_buf, perm_buf, up_buf, ffn_halo, act_buf):
    s = pl.program_id(1)
    T = TILE

    @pl.when(s == 0)
    def _():
        k_ext[:, 0:BLOCK, :] = jnp.zeros((4, BLOCK, LANES), BF16)
        v_ext[:, 0:BLOCK, :] = jnp.zeros((4, BLOCK, 2 * LANES), BF16)
        u_buf[0:HALO, :] = jnp.zeros((HALO, CONV_WIDTH), F32)
        ffn_halo[...] = jnp.zeros((HEAD, 2 * D_FF), F32)

    @pl.when(s > 0)
    def _():
        for e in range(4):
            k_ext[e, 0:BLOCK, :] = k_ext[e, T:T + BLOCK, :]
            v_ext[e, 0:BLOCK, :] = v_ext[e, T:T + BLOCK, :]

    x = x_ref[...]
    xn = _rms(x, mixg_ref[...]).astype(BF16)

    def proj(lo, width):
        return _dot(xn, w_in_ref[:, lo:lo + width]) + b_in_ref[:, lo:lo + width]

    q_buf[...] = (proj(Q_OFF, ATTN_WIDTH) * ATTN_SCALE).astype(BF16)
    k = proj(K_OFF, KV_WIDTH)
    v = proj(V_OFF, KV_WIDTH)
    lane = lax.broadcasted_iota(jnp.int32, (T, LANES), 1)
    lo_half = lane < HEAD_DIM
    k_sw = pltpu.roll(k, HEAD_DIM, axis=1)
    v_sw = pltpu.roll(v, HEAD_DIM, axis=1)
    zero = jnp.zeros((T, LANES), F32)
    one = jnp.ones((T, LANES), F32)
    ones_lo = jnp.where(lo_half, one, zero).astype(BF16)
    ones_hi = jnp.where(lo_half, zero, one).astype(BF16)
    rows = slice(BLOCK, BLOCK + T)
    for h, (t_lo, t_hi) in enumerate(((k, k_sw), (k_sw, k))):
        k_ext[2 * h, rows, :] = jnp.where(lo_half, t_lo, zero).astype(BF16)
        k_ext[2 * h + 1, rows, :] = jnp.where(lo_half, zero, t_hi).astype(BF16)
    for h, (t_lo, t_hi) in enumerate(((v, v_sw), (v_sw, v))):
        v_ext[2 * h, rows, 0:LANES] = jnp.where(lo_half, t_lo, zero).astype(BF16)
        v_ext[2 * h, rows, LANES:] = ones_lo
        v_ext[2 * h + 1, rows, 0:LANES] = jnp.where(lo_half, zero, t_hi).astype(BF16)
        v_ext[2 * h + 1, rows, LANES:] = ones_hi

    qi = lax.broadcasted_iota(jnp.int32, (2 * BLOCK, 2 * BLOCK), 0) % BLOCK
    kj = lax.broadcasted_iota(jnp.int32, (2 * BLOCK, 2 * BLOCK), 1)
    dist = qi + BLOCK - kj
    band = (dist >= 0) & (dist < WINDOW)
    first_key = jnp.where(s == 0, BLOCK, 0)
    band_first = band & (kj >= first_key)
    top_rows = lax.broadcasted_iota(jnp.int32, (2 * BLOCK, 1), 0) < BLOCK
    lo_half_o = lax.broadcasted_iota(jnp.int32, (2 * BLOCK, LANES), 1) < HEAD_DIM

    for j in range(T // BLOCK):
        mask = band_first if j == 0 else band
        qrows = slice(j * BLOCK, (j + 1) * BLOCK)
        krows = slice(j * BLOCK, (j + 2) * BLOCK)
        for h in range(N_KV_HEADS):
            qq = jnp.concatenate([q_buf[qrows, (2 * h) * LANES:(2 * h + 1) * LANES],
                                  q_buf[qrows, (2 * h + 1) * LANES:(2 * h + 2) * LANES]], axis=0)
            acc = None
            esink = []
            for half in range(2):
                e = 2 * h + half
                sc = _dot_nt(qq, k_ext[e, krows, :])
                sc = jnp.where(mask, sc, -jnp.inf)
                sink = jnp.where(top_rows, sinks_ref[GROUP * h + half], sinks_ref[GROUP * h + 2 + half])
                m = jnp.maximum(jnp.max(sc, axis=-1, keepdims=True), sink)
                p = jnp.exp(sc - m).astype(BF16)
                pv = _dot(p, v_ext[e, krows, :])
                acc = pv if acc is None else acc + pv
                esink.append(jnp.exp(sink - m))
            den = acc[:, LANES:] + jnp.where(lo_half_o, esink[0], esink[1])
            o = (acc[:, :LANES] / den).astype(BF16)
            attn_buf[qrows, (2 * h) * LANES:(2 * h + 1) * LANES] = o[:BLOCK]
            attn_buf[qrows, (2 * h + 1) * LANES:(2 * h + 2) * LANES] = o[BLOCK:]

    u = proj(CC_OFF, CONV_WIDTH) * proj(CX_OFF, CONV_WIDTH)
    u_buf[HALO:HALO + T, :] = u
    conv = (convw_ref[0:1, :] * u_buf[HALO - 2:HALO - 2 + T, :]
            + convw_ref[1:2, :] * u_buf[HALO - 1:HALO - 1 + T, :]
            + convw_ref[2:3, :] * u)
    conv = (proj(CB_OFF, CONV_WIDTH) * conv).astype(BF16)
    u_buf[0:HALO, :] = u_buf[T:T + HALO, :]

    merged = (jax.nn.sigmoid(proj(GA_OFF, D_MODEL)) * _dot(attn_buf[...], w_attn_ref[...])
              + jax.nn.sigmoid(proj(GC_OFF, D_MODEL)) * _dot(conv, w_conv_ref[...]))
    hmid = x + _dot(merged.astype(BF16), w_out_ref[...])
    o_ref[...] = hmid

    hn = _rms(hmid, ffng_ref[...])
    for a in range(T // SUBLANES):
        dst = pl.ds(RUN * (a % SUBLANES) + a // SUBLANES, SUBLANES, stride=SUBLANES)
        for c in range(D_MODEL // LANES):
            perm_buf[c, dst, :] = hn[a * SUBLANES:(a + 1) * SUBLANES, c * LANES:(c + 1) * LANES]
    hn_il = jnp.concatenate([perm_buf[c] for c in range(D_MODEL // LANES)], axis=1).astype(BF16)

    first_sub = lax.broadcasted_iota(jnp.int32, (SUBLANES, FFN_CHUNK), 0) == 0
    for c in range(D_FF // FFN_CHUNK):
        buf = up_buf.at[c % 2]
        conved = []
        for part, off in enumerate((c * FFN_CHUNK, D_FF + c * FFN_CHUNK)):
            cols = slice(off, off + FFN_CHUNK)
            bcols = slice(part * FFN_CHUNK, (part + 1) * FFN_CHUNK)
            buf[HEAD:HEAD + T, bcols] = _dot(hn_il, w_up_ref[:, cols])
            for j in range(HEAD // SUBLANES):
                vrows = slice(j * SUBLANES, (j + 1) * SUBLANES)
                cur = buf[T + j * SUBLANES:T + (j + 1) * SUBLANES, bcols]
                buf[vrows, bcols] = jnp.where(first_sub, pltpu.roll(ffn_halo[vrows, cols], 1, axis=0),
                                              pltpu.roll(cur, 1, axis=0))
                ffn_halo[vrows, cols] = cur
            conved.append(ffncw_ref[0:1, cols] * buf[0:T, bcols]
                          + ffncw_ref[1:2, cols] * buf[SUBLANES:SUBLANES + T, bcols]
                          + ffncw_ref[2:3, cols] * buf[HEAD:HEAD + T, bcols])
        gate, val = conved
        act_buf[:, c * FFN_CHUNK:(c + 1) * FFN_CHUNK] = (jax.nn.silu(gate) * val).astype(BF16)

    down = _dot(act_buf[...], w_down_ref[...])
    for c in range(D_MODEL // LANES):
        perm_buf[c] = down[:, c * LANES:(c + 1) * LANES]
    for a in range(T // SUBLANES):
        src = pl.ds(RUN * (a % SUBLANES) + a // SUBLANES, SUBLANES, stride=SUBLANES)
        for c in range(D_MODEL // LANES):
            o_ref[a * SUBLANES:(a + 1) * SUBLANES, c * LANES:(c + 1) * LANES] += perm_buf[c, src, :]
    o_ref[...] = _rms(o_ref[...], fing_ref[...])


def _resident(shape):
    return pl.BlockSpec(shape, lambda b, s: (0,) * len(shape), pipeline_mode=pl.Buffered(1))


@jax.jit
def kernel(x, mix_norm, w_in, b_in, sinks, conv_w, w_attn_branch, w_conv_branch, w_out, ffn_norm, w_up, ffn_conv_w,
           w_down, final_norm):
    batch, seq, d_model = x.shape
    assert d_model == D_MODEL and seq % TILE == 0 and mix_norm.shape[0] == 1
    T = TILE
    tile_spec = pl.BlockSpec((None, T, D_MODEL), lambda b, s: (b, s, 0))
    in_specs = [
        pl.BlockSpec(memory_space=pltpu.SMEM),
        tile_spec,
        _resident((1, D_MODEL)),
        _resident((D_MODEL, IN_WIDTH)),
        _resident((1, IN_WIDTH)),
        _resident((CONV_K, CONV_WIDTH)),
        _resident((ATTN_WIDTH, D_MODEL)),
        _resident((CONV_WIDTH, D_MODEL)),
        _resident((D_MODEL, D_MODEL)),
        _resident((1, D_MODEL)),
        _resident((D_MODEL, 2 * D_FF)),
        _resident((CONV_K, 2 * D_FF)),
        _resident((D_FF, D_MODEL)),
        _resident((1, D_MODEL)),
    ]
    scratch = [
        pltpu.VMEM((T, ATTN_WIDTH), BF16),
        pltpu.VMEM((4, T + BLOCK, LANES), BF16),
        pltpu.VMEM((4, T + BLOCK, 2 * LANES), BF16),
        pltpu.VMEM((T, ATTN_WIDTH), BF16),
        pltpu.VMEM((T + HALO, CONV_WIDTH), F32),
        pltpu.VMEM((D_MODEL // LANES, T, LANES), F32),
        pltpu.VMEM((2, T + HEAD, 2 * FFN_CHUNK), F32),
        pltpu.VMEM((HEAD, 2 * D_FF), F32),
        pltpu.VMEM((T, D_FF), BF16),
    ]
    call = pl.pallas_call(
        _layer_kernel,
        grid=(batch, seq // T),
        in_specs=in_specs,
        out_specs=tile_spec,
        out_shape=jax.ShapeDtypeStruct(x.shape, F32),
        scratch_shapes=scratch,
        compiler_params=pltpu.CompilerParams(
            dimension_semantics=("arbitrary", "arbitrary"),
            vmem_limit_bytes=VMEM_LIMIT_BYTES),
        name="hybrid_layer",
    )
    return call(
        sinks[0], x, mix_norm, w_in[0].astype(BF16), b_in, conv_w[0],
        w_attn_branch[0].astype(BF16), w_conv_branch[0].astype(BF16), w_out[0].astype(BF16),
        ffn_norm, w_up[0].astype(BF16), ffn_conv_w[0], w_down[0].astype(BF16), final_norm.reshape(1, D_MODEL))
```

```python
import jax
import jax.numpy as jnp
from jax import lax
from jax.experimental import pallas as pl
from jax.experimental.pallas import tpu as pltpu

D_MODEL = 1024
HEAD_DIM = 64
N_HEADS = 8
N_KV_HEADS = 2
GROUP = N_HEADS // N_KV_HEADS
WINDOW = 128
BLOCK = 128
ATTN_SCALE = HEAD_DIM ** -0.5
ATTN_WIDTH = N_HEADS * HEAD_DIM
KV_WIDTH = N_KV_HEADS * HEAD_DIM
CONV_WIDTH = 512
CONV_K = 3
IN_WIDTH = ATTN_WIDTH + 2 * KV_WIDTH + 3 * CONV_WIDTH + 2 * D_MODEL
D_FF = 2816
NORM_EPS = 1e-5

Q_OFF = 0
K_OFF = ATTN_WIDTH
V_OFF = K_OFF + KV_WIDTH
CB_OFF = V_OFF + KV_WIDTH
CC_OFF = CB_OFF + CONV_WIDTH
CX_OFF = CC_OFF + CONV_WIDTH
GA_OFF = CX_OFF + CONV_WIDTH
GC_OFF = GA_OFF + D_MODEL

LANES = 128
SUBLANES = 8
HALO = SUBLANES
TILE = 512
N_SUB = 2
SUB = TILE // N_SUB
RUN = SUB // SUBLANES
HEAD = (3 - 1) * SUBLANES
FFN_CHUNK = 256
VMEM_LIMIT_BYTES = 58 * 1024 * 1024

F32 = jnp.float32
BF16 = jnp.bfloat16


def _rms(x, g):
    ms = jnp.mean(x * x, axis=-1, keepdims=True)
    return x * lax.rsqrt(ms + NORM_EPS) * g


def _dot(a, b):
    return jnp.dot(a, b, preferred_element_type=F32)


def _dot_nt(a, b):
    return lax.dot_general(a, b, (((1,), (1,)), ((), ())), preferred_element_type=F32)


def _interleaved_rows(a):
    per_run = RUN // SUBLANES
    return pl.ds(SUBLANES * SUBLANES * (a % per_run) + a // per_run, SUBLANES, stride=SUBLANES)


def _layer_kernel(sinks_ref, x_ref, mixg_ref, w_in_ref, b_in_ref, convw_ref, w_attn_ref, w_conv_ref, w_out_ref,
                  ffng_ref, w_up_ref, ffncw_ref, w_down_ref, fing_ref, o_ref,
                  xn_buf, q_buf, k_ext, v_ext, attn_buf, u_buf, conv_buf, hn_buf, perm_buf, up_buf, ffn_halo,
                  act_buf):
    s = pl.program_id(1)
    T = TILE
    n_slabs = D_MODEL // LANES

    @pl.when(s == 0)
    def _():
        k_ext[:, 0:BLOCK, :] = jnp.zeros((4, BLOCK, LANES), BF16)
        v_ext[:, 0:BLOCK, :] = jnp.zeros((4, BLOCK, 2 * LANES), BF16)
        u_buf[0:HALO, :] = jnp.zeros((HALO, CONV_WIDTH), F32)
        ffn_halo[...] = jnp.zeros((HEAD, 2 * D_FF), F32)

    @pl.when(s > 0)
    def _():
        for e in range(4):
            k_ext[e, 0:BLOCK, :] = k_ext[e, T:T + BLOCK, :]
            v_ext[e, 0:BLOCK, :] = v_ext[e, T:T + BLOCK, :]
        u_buf[0:HALO, :] = u_buf[T:T + HALO, :]

    def rows_of(u):
        return slice(u * SUB, (u + 1) * SUB)

    def norm_in(u):
        xn_buf[rows_of(u), :] = _rms(x_ref[rows_of(u), :], mixg_ref[...]).astype(BF16)

    def proj(u, lo, width):
        return _dot(xn_buf[rows_of(u), :], w_in_ref[:, lo:lo + width]) + b_in_ref[:, lo:lo + width]

    lane = lax.broadcasted_iota(jnp.int32, (SUB, LANES), 1)
    lo_half = lane < HEAD_DIM
    zero = jnp.zeros((SUB, LANES), F32)
    one = jnp.ones((SUB, LANES), F32)
    ones_lo = jnp.where(lo_half, one, zero).astype(BF16)
    ones_hi = jnp.where(lo_half, zero, one).astype(BF16)

    def qkv(u):
        q_buf[rows_of(u), :] = (proj(u, Q_OFF, ATTN_WIDTH) * ATTN_SCALE).astype(BF16)
        kv = proj(u, K_OFF, 2 * KV_WIDTH)
        k = kv[:, :KV_WIDTH]
        v = kv[:, KV_WIDTH:]
        k_sw = pltpu.roll(k, HEAD_DIM, axis=1)
        v_sw = pltpu.roll(v, HEAD_DIM, axis=1)
        rows = slice(BLOCK + u * SUB, BLOCK + (u + 1) * SUB)
        for h, (t_lo, t_hi) in enumerate(((k, k_sw), (k_sw, k))):
            k_ext[2 * h, rows, :] = jnp.where(lo_half, t_lo, zero).astype(BF16)
            k_ext[2 * h + 1, rows, :] = jnp.where(lo_half, zero, t_hi).astype(BF16)
        for h, (t_lo, t_hi) in enumerate(((v, v_sw), (v_sw, v))):
            v_ext[2 * h, rows, 0:LANES] = jnp.where(lo_half, t_lo, zero).astype(BF16)
            v_ext[2 * h, rows, LANES:] = ones_lo
            v_ext[2 * h + 1, rows, 0:LANES] = jnp.where(lo_half, zero, t_hi).astype(BF16)
            v_ext[2 * h + 1, rows, LANES:] = ones_hi

    qi = lax.broadcasted_iota(jnp.int32, (2 * BLOCK, 2 * BLOCK), 0) % BLOCK
    kj = lax.broadcasted_iota(jnp.int32, (2 * BLOCK, 2 * BLOCK), 1)
    dist = qi + BLOCK - kj
    band = (dist >= 0) & (dist < WINDOW)
    first_key = jnp.where(s == 0, BLOCK, 0)
    band_first = band & (kj >= first_key)
    top_rows = lax.broadcasted_iota(jnp.int32, (2 * BLOCK, 1), 0) < BLOCK
    lo_half_o = lax.broadcasted_iota(jnp.int32, (2 * BLOCK, LANES), 1) < HEAD_DIM

    def attn_block(j, h):
        mask = band_first if j == 0 else band
        qrows = slice(j * BLOCK, (j + 1) * BLOCK)
        krows = slice(j * BLOCK, (j + 2) * BLOCK)
        qq = jnp.concatenate([q_buf[qrows, (2 * h) * LANES:(2 * h + 1) * LANES],
                              q_buf[qrows, (2 * h + 1) * LANES:(2 * h + 2) * LANES]], axis=0)
        acc = None
        esink = []
        for half in range(2):
            e = 2 * h + half
            sc = _dot_nt(qq, k_ext[e, krows, :])
            sc = jnp.where(mask, sc, -jnp.inf)
            sink = jnp.where(top_rows, sinks_ref[GROUP * h + half], sinks_ref[GROUP * h + 2 + half])
            m = jnp.maximum(jnp.max(sc, axis=-1, keepdims=True), sink)
            p = jnp.exp(sc - m).astype(BF16)
            pv = _dot(p, v_ext[e, krows, :])
            acc = pv if acc is None else acc + pv
            esink.append(jnp.exp(sink - m))
        den = acc[:, LANES:] + jnp.where(lo_half_o, esink[0], esink[1])
        o = (acc[:, :LANES] / den).astype(BF16)
        attn_buf[qrows, (2 * h) * LANES:(2 * h + 1) * LANES] = o[:BLOCK]
        attn_buf[qrows, (2 * h + 1) * LANES:(2 * h + 2) * LANES] = o[BLOCK:]

    def attention(u):
        for j in range(u * SUB // BLOCK, (u + 1) * SUB // BLOCK):
            for h in range(N_KV_HEADS):
                attn_block(j, h)

    def short_conv(u):
        lo = HALO + u * SUB
        uu = proj(u, CC_OFF, CONV_WIDTH) * proj(u, CX_OFF, CONV_WIDTH)
        u_buf[lo:lo + SUB, :] = uu
        conv = (convw_ref[0:1, :] * u_buf[lo - 2:lo - 2 + SUB, :]
                + convw_ref[1:2, :] * u_buf[lo - 1:lo - 1 + SUB, :]
                + convw_ref[2:3, :] * uu)
        conv_buf[rows_of(u), :] = (proj(u, CB_OFF, CONV_WIDTH) * conv).astype(BF16)

    def merge(u):
        r = rows_of(u)
        merged = (jax.nn.sigmoid(proj(u, GA_OFF, D_MODEL)) * _dot(attn_buf[r, :], w_attn_ref[...])
                  + jax.nn.sigmoid(proj(u, GC_OFF, D_MODEL)) * _dot(conv_buf[r, :], w_conv_ref[...]))
        o_ref[r, :] = x_ref[r, :] + _dot(merged.astype(BF16), w_out_ref[...])

    def norm_mid(u):
        hn = _rms(o_ref[rows_of(u), :], ffng_ref[...])
        for a in range(SUB // SUBLANES):
            dst = _interleaved_rows(a)
            for c in range(n_slabs):
                perm_buf[u * n_slabs + c, dst, :] = hn[a * SUBLANES:(a + 1) * SUBLANES, c * LANES:(c + 1) * LANES]
        hn_buf[rows_of(u), :] = jnp.concatenate(
            [perm_buf[u * n_slabs + c] for c in range(n_slabs)], axis=1).astype(BF16)

    first_sub = lax.broadcasted_iota(jnp.int32, (SUBLANES, FFN_CHUNK), 0) == 0

    def ffn_chunk(u, c):
        buf = up_buf.at[u, c % 2]
        conved = []
        for part, off in enumerate((c * FFN_CHUNK, D_FF + c * FFN_CHUNK)):
            cols = slice(off, off + FFN_CHUNK)
            bcols = slice(part * FFN_CHUNK, (part + 1) * FFN_CHUNK)
            buf[HEAD:HEAD + SUB, bcols] = _dot(hn_buf[rows_of(u), :], w_up_ref[:, cols])
            for j in range(HEAD // SUBLANES):
                vrows = slice(j * SUBLANES, (j + 1) * SUBLANES)
                cur = buf[SUB + j * SUBLANES:SUB + (j + 1) * SUBLANES, bcols]
                buf[vrows, bcols] = jnp.where(first_sub, pltpu.roll(ffn_halo[vrows, cols], 1, axis=0),
                                              pltpu.roll(cur, 1, axis=0))
                ffn_halo[vrows, cols] = cur
            conved.append(ffncw_ref[0:1, cols] * buf[0:SUB, bcols]
                          + ffncw_ref[1:2, cols] * buf[SUBLANES:SUBLANES + SUB, bcols]
                          + ffncw_ref[2:3, cols] * buf[HEAD:HEAD + SUB, bcols])
        gate, val = conved
        act_buf[rows_of(u), c * FFN_CHUNK:(c + 1) * FFN_CHUNK] = (jax.nn.silu(gate) * val).astype(BF16)

    def down(u):
        d = _dot(act_buf[rows_of(u), :], w_down_ref[...])
        for c in range(n_slabs):
            perm_buf[u * n_slabs + c] = d[:, c * LANES:(c + 1) * LANES]

    def finish(u):
        groups = []
        for a in range(SUB // SUBLANES):
            src = _interleaved_rows(a)
            groups.append(jnp.concatenate([perm_buf[u * n_slabs + c, src, :] for c in range(n_slabs)], axis=1))
        hout = o_ref[rows_of(u), :] + jnp.concatenate(groups, axis=0)
        o_ref[rows_of(u), :] = _rms(hout, fing_ref[...])

    for phase in (norm_in, qkv, attention, short_conv, merge, norm_mid):
        for u in range(N_SUB):
            phase(u)
    for c in range(D_FF // FFN_CHUNK):
        for u in range(N_SUB):
            ffn_chunk(u, c)
    for phase in (down, finish):
        for u in range(N_SUB):
            phase(u)


def _resident(shape):
    return pl.BlockSpec(shape, lambda b, s: (0,) * len(shape), pipeline_mode=pl.Buffered(1))


@jax.jit
def kernel(x, mix_norm, w_in, b_in, sinks, conv_w, w_attn_branch, w_conv_branch, w_out, ffn_norm, w_up, ffn_conv_w,
           w_down, final_norm):
    batch, seq, d_model = x.shape
    assert d_model == D_MODEL and seq % TILE == 0 and mix_norm.shape[0] == 1
    T = TILE
    tile_spec = pl.BlockSpec((None, T, D_MODEL), lambda b, s: (b, s, 0))
    in_specs = [
        pl.BlockSpec(memory_space=pltpu.SMEM),
        tile_spec,
        _resident((1, D_MODEL)),
        _resident((D_MODEL, IN_WIDTH)),
        _resident((1, IN_WIDTH)),
        _resident((CONV_K, CONV_WIDTH)),
        _resident((ATTN_WIDTH, D_MODEL)),
        _resident((CONV_WIDTH, D_MODEL)),
        _resident((D_MODEL, D_MODEL)),
        _resident((1, D_MODEL)),
        _resident((D_MODEL, 2 * D_FF)),
        _resident((CONV_K, 2 * D_FF)),
        _resident((D_FF, D_MODEL)),
        _resident((1, D_MODEL)),
    ]
    scratch = [
        pltpu.VMEM((T, D_MODEL), BF16),
        pltpu.VMEM((T, ATTN_WIDTH), BF16),
        pltpu.VMEM((4, T + BLOCK, LANES), BF16),
        pltpu.VMEM((4, T + BLOCK, 2 * LANES), BF16),
        pltpu.VMEM((T, ATTN_WIDTH), BF16),
        pltpu.VMEM((T + HALO, CONV_WIDTH), F32),
        pltpu.VMEM((T, CONV_WIDTH), BF16),
        pltpu.VMEM((T, D_MODEL), BF16),
        pltpu.VMEM((N_SUB * D_MODEL // LANES, SUB, LANES), F32),
        pltpu.VMEM((N_SUB, 2, SUB + HEAD, 2 * FFN_CHUNK), F32),
        pltpu.VMEM((HEAD, 2 * D_FF), F32),
        pltpu.VMEM((T, D_FF), BF16),
    ]
    call = pl.pallas_call(
        _layer_kernel,
        grid=(batch, seq // T),
        in_specs=in_specs,
        out_specs=tile_spec,
        out_shape=jax.ShapeDtypeStruct(x.shape, F32),
        scratch_shapes=scratch,
        compiler_params=pltpu.CompilerParams(
            dimension_semantics=("arbitrary", "arbitrary"),
            vmem_limit_bytes=VMEM_LIMIT_BYTES),
        name="hybrid_layer",
    )
    return call(
        sinks[0], x, mix_norm, w_in[0].astype(BF16), b_in, conv_w[0],
        w_attn_branch[0].astype(BF16), w_conv_branch[0].astype(BF16), w_out[0].astype(BF16),
        ffn_norm, w_up[0].astype(BF16), ffn_conv_w[0], w_down[0].astype(BF16), final_norm.reshape(1, D_MODEL))
```

```python
import jax
import jax.numpy as jnp
from jax import lax
from jax.experimental import pallas as pl
from jax.experimental.pallas import tpu as pltpu

D_MODEL = 1024
HEAD_DIM = 64
N_HEADS = 8
N_KV_HEADS = 2
GROUP = N_HEADS // N_KV_HEADS
WINDOW = 128
BLOCK = 128
ATTN_SCALE = HEAD_DIM ** -0.5
ATTN_WIDTH = N_HEADS * HEAD_DIM
KV_WIDTH = N_KV_HEADS * HEAD_DIM
CONV_WIDTH = 512
CONV_K = 3
IN_WIDTH = ATTN_WIDTH + 2 * KV_WIDTH + 3 * CONV_WIDTH + 2 * D_MODEL
D_FF = 2816
NORM_EPS = 1e-5
MASKED = -1e30

Q_OFF = 0
K_OFF = ATTN_WIDTH
V_OFF = K_OFF + KV_WIDTH
CB_OFF = V_OFF + KV_WIDTH
CC_OFF = CB_OFF + CONV_WIDTH
CX_OFF = CC_OFF + CONV_WIDTH
GA_OFF = CX_OFF + CONV_WIDTH
GC_OFF = GA_OFF + D_MODEL

LANES = 128
SUBLANES = 8
HALO = SUBLANES
TILE = 512
N_SUB = 2
SUB = TILE // N_SUB
RUN = SUB // SUBLANES
HEAD = (3 - 1) * SUBLANES
FFN_CHUNK = 256
WIDE_CHUNK_ROWS = 64
TALL_CHUNK_ROWS = 256
WIDE_SLOTS = 4
TALL_SLOTS = 3
VMEM_LIMIT_BYTES = 58 * 1024 * 1024

F32 = jnp.float32
BF16 = jnp.bfloat16


def _rms(x, g):
    ms = jnp.mean(x * x, axis=-1, keepdims=True)
    return x * lax.rsqrt(ms + NORM_EPS) * g


def _dot(a, b):
    return jnp.dot(a, b, preferred_element_type=F32)


def _dot_nt(a, b):
    return lax.dot_general(a, b, (((1,), (1,)), ((), ())), preferred_element_type=F32)


def _interleaved_rows(a):
    per_run = RUN // SUBLANES
    return pl.ds(SUBLANES * SUBLANES * (a % per_run) + a // per_run, SUBLANES, stride=SUBLANES)


class _WeightStream:
    def __init__(self, w_hbm, w_ref, stage, sem, chunk_rows):
        self.w_hbm, self.w_ref, self.stage, self.sem, self.chunk_rows = w_hbm, w_ref, stage, sem, chunk_rows
        self.cols = w_hbm.shape[1]
        self.n_chunks = w_hbm.shape[0] // chunk_rows
        self.n_slots = stage.shape[0]
        self.ahead = min(self.n_slots - 1, self.n_chunks)

    def _copy(self, i, slot):
        return pltpu.make_async_copy(self.w_hbm.at[pl.ds(i * self.chunk_rows, self.chunk_rows), :],
                                     self.stage.at[slot, :, pl.ds(0, self.cols)], self.sem.at[slot])

    def prime(self):
        for i in range(self.ahead):
            self._copy(i, i).start()

    def drain(self):
        def body(i, carry):
            @pl.when(i + self.ahead < self.n_chunks)
            def _():
                self._copy(i + self.ahead, (i + self.ahead) % self.n_slots).start()

            slot = i % self.n_slots
            self._copy(i, slot).wait()
            dst = pl.ds(pl.multiple_of(i * self.chunk_rows, self.chunk_rows), self.chunk_rows)
            self.w_ref[dst, :] = self.stage[slot, :, 0:self.cols].astype(BF16)
            return carry

        lax.fori_loop(0, self.n_chunks, body, 0)


def _layer_kernel(sinks_ref, x_ref, mixg_ref, w_in_hbm, b_in_ref, convw_ref, w_attn_hbm, w_conv_hbm, w_out_hbm,
                  ffng_ref, w_up_hbm, ffncw_ref, w_down_hbm, fing_ref, o_ref,
                  w_in_ref, w_attn_ref, w_conv_ref, w_out_ref, w_up_ref, w_down_ref, stage_wide, stage_tall, wide_sem,
                  tall_sem,
                  xn_buf, q_buf, k_ext, v_ext, attn_buf, u_buf, conv_buf, hn_buf, perm_buf, up_buf, ffn_halo,
                  act_buf):
    s = pl.program_id(1)
    T = TILE
    n_slabs = D_MODEL // LANES

    @pl.when((pl.program_id(0) == 0) & (s == 0))
    def _():
        w_in = _WeightStream(w_in_hbm, w_in_ref, stage_wide, wide_sem, WIDE_CHUNK_ROWS)
        w_up = _WeightStream(w_up_hbm, w_up_ref, stage_wide, wide_sem, WIDE_CHUNK_ROWS)
        w_attn = _WeightStream(w_attn_hbm, w_attn_ref, stage_tall, tall_sem, TALL_CHUNK_ROWS)
        w_conv = _WeightStream(w_conv_hbm, w_conv_ref, stage_tall, tall_sem, TALL_CHUNK_ROWS)
        w_out = _WeightStream(w_out_hbm, w_out_ref, stage_tall, tall_sem, TALL_CHUNK_ROWS)
        w_down = _WeightStream(w_down_hbm, w_down_ref, stage_tall, tall_sem, TALL_CHUNK_ROWS)
        w_in.prime()
        w_attn.prime()
        w_in.drain()
        w_up.prime()
        w_attn.drain()
        w_conv.prime()
        w_conv.drain()
        w_out.prime()
        w_out.drain()
        w_down.prime()
        w_up.drain()
        w_down.drain()

    @pl.when(s == 0)
    def _():
        k_ext[:, 0:BLOCK, :] = jnp.zeros((4, BLOCK, LANES), BF16)
        v_ext[:, 0:BLOCK, :] = jnp.zeros((4, BLOCK, 2 * LANES), BF16)
        u_buf[0:HALO, :] = jnp.zeros((HALO, CONV_WIDTH), F32)
        ffn_halo[...] = jnp.zeros((HEAD, 2 * D_FF), F32)

    @pl.when(s > 0)
    def _():
        for e in range(4):
            k_ext[e, 0:BLOCK, :] = k_ext[e, T:T + BLOCK, :]
            v_ext[e, 0:BLOCK, :] = v_ext[e, T:T + BLOCK, :]
        u_buf[0:HALO, :] = u_buf[T:T + HALO, :]

    def rows_of(u):
        return slice(u * SUB, (u + 1) * SUB)

    def norm_in(u):
        xn_buf[rows_of(u), :] = _rms(x_ref[rows_of(u), :], mixg_ref[...]).astype(BF16)

    def proj(u, lo, width):
        return _dot(xn_buf[rows_of(u), :], w_in_ref[:, lo:lo + width]) + b_in_ref[:, lo:lo + width]

    lane = lax.broadcasted_iota(jnp.int32, (SUB, LANES), 1)
    lo_half = lane < HEAD_DIM
    zero = jnp.zeros((SUB, LANES), F32)
    one = jnp.ones((SUB, LANES), F32)
    ones_lo = jnp.where(lo_half, one, zero).astype(BF16)
    ones_hi = jnp.where(lo_half, zero, one).astype(BF16)

    def qkv(u):
        q_buf[rows_of(u), :] = (proj(u, Q_OFF, ATTN_WIDTH) * ATTN_SCALE).astype(BF16)
        kv = proj(u, K_OFF, 2 * KV_WIDTH)
        k = kv[:, :KV_WIDTH]
        v = kv[:, KV_WIDTH:]
        k_sw = pltpu.roll(k, HEAD_DIM, axis=1)
        v_sw = pltpu.roll(v, HEAD_DIM, axis=1)
        rows = slice(BLOCK + u * SUB, BLOCK + (u + 1) * SUB)
        for h, (t_lo, t_hi) in enumerate(((k, k_sw), (k_sw, k))):
            k_ext[2 * h, rows, :] = jnp.where(lo_half, t_lo, zero).astype(BF16)
            k_ext[2 * h + 1, rows, :] = jnp.where(lo_half, zero, t_hi).astype(BF16)
        for h, (t_lo, t_hi) in enumerate(((v, v_sw), (v_sw, v))):
            v_ext[2 * h, rows, 0:LANES] = jnp.where(lo_half, t_lo, zero).astype(BF16)
            v_ext[2 * h, rows, LANES:] = ones_lo
            v_ext[2 * h + 1, rows, 0:LANES] = jnp.where(lo_half, zero, t_hi).astype(BF16)
            v_ext[2 * h + 1, rows, LANES:] = ones_hi

    q_pos = lax.broadcasted_iota(jnp.int32, (2 * BLOCK, BLOCK), 0) % BLOCK
    q_onehot = (lax.broadcasted_iota(jnp.int32, (2 * BLOCK, BLOCK), 1) == q_pos).astype(BF16)
    kj = lax.broadcasted_iota(jnp.int32, (2 * BLOCK, BLOCK), 0)
    dist = lax.broadcasted_iota(jnp.int32, (2 * BLOCK, BLOCK), 1) + BLOCK - kj
    in_band = (dist >= 0) & (dist < WINDOW)
    first_key = jnp.where(s == 0, BLOCK, 0)
    key_bias = jnp.where(in_band, 0.0, MASKED).astype(BF16)
    key_bias_first = jnp.where(in_band & (kj >= first_key), 0.0, MASKED).astype(BF16)
    top_rows = lax.broadcasted_iota(jnp.int32, (2 * BLOCK, 1), 0) < BLOCK
    lo_half_o = lax.broadcasted_iota(jnp.int32, (2 * BLOCK, LANES), 1) < HEAD_DIM

    def attn_block(j, h):
        bias = key_bias_first if j == 0 else key_bias
        qrows = slice(j * BLOCK, (j + 1) * BLOCK)
        krows = slice(j * BLOCK, (j + 2) * BLOCK)
        qq = jnp.concatenate([q_buf[qrows, (2 * h) * LANES:(2 * h + 1) * LANES],
                              q_buf[qrows, (2 * h + 1) * LANES:(2 * h + 2) * LANES]], axis=0)
        qq = jnp.concatenate([qq, q_onehot], axis=1)
        acc = None
        esink = []
        for half in range(2):
            e = 2 * h + half
            sc = _dot_nt(qq, jnp.concatenate([k_ext[e, krows, :], bias], axis=1))
            sink = jnp.where(top_rows, sinks_ref[GROUP * h + half], sinks_ref[GROUP * h + 2 + half])
            m = jnp.maximum(jnp.max(sc, axis=-1, keepdims=True), sink)
            p = jnp.exp(sc - m).astype(BF16)
            pv = _dot(p, v_ext[e, krows, :])
            acc = pv if acc is None else acc + pv
            esink.append(jnp.exp(sink - m))
        den = acc[:, LANES:] + jnp.where(lo_half_o, esink[0], esink[1])
        o = (acc[:, :LANES] / den).astype(BF16)
        attn_buf[qrows, (2 * h) * LANES:(2 * h + 1) * LANES] = o[:BLOCK]
        attn_buf[qrows, (2 * h + 1) * LANES:(2 * h + 2) * LANES] = o[BLOCK:]

    def attention(u):
        for j in range(u * SUB // BLOCK, (u + 1) * SUB // BLOCK):
            for h in range(N_KV_HEADS):
                attn_block(j, h)

    def short_conv(u):
        lo = HALO + u * SUB
        uu = proj(u, CC_OFF, CONV_WIDTH) * proj(u, CX_OFF, CONV_WIDTH)
        u_buf[lo:lo + SUB, :] = uu
        conv = (convw_ref[0:1, :] * u_buf[lo - 2:lo - 2 + SUB, :]
                + convw_ref[1:2, :] * u_buf[lo - 1:lo - 1 + SUB, :]
                + convw_ref[2:3, :] * uu)
        conv_buf[rows_of(u), :] = (proj(u, CB_OFF, CONV_WIDTH) * conv).astype(BF16)

    def merge(u):
        r = rows_of(u)
        merged = (jax.nn.sigmoid(proj(u, GA_OFF, D_MODEL)) * _dot(attn_buf[r, :], w_attn_ref[...])
                  + jax.nn.sigmoid(proj(u, GC_OFF, D_MODEL)) * _dot(conv_buf[r, :], w_conv_ref[...]))
        o_ref[r, :] = x_ref[r, :] + _dot(merged.astype(BF16), w_out_ref[...])

    def norm_mid(u):
        hn = _rms(o_ref[rows_of(u), :], ffng_ref[...])
        for a in range(SUB // SUBLANES):
            dst = _interleaved_rows(a)
            for c in range(n_slabs):
                perm_buf[u * n_slabs + c, dst, :] = hn[a * SUBLANES:(a + 1) * SUBLANES, c * LANES:(c + 1) * LANES]
        hn_buf[rows_of(u), :] = jnp.concatenate(
            [perm_buf[u * n_slabs + c] for c in range(n_slabs)], axis=1).astype(BF16)

    first_sub = lax.broadcasted_iota(jnp.int32, (SUBLANES, FFN_CHUNK), 0) == 0

    def ffn_chunk(u, c):
        buf = up_buf.at[u, c % 2]
        conved = []
        for part, off in enumerate((c * FFN_CHUNK, D_FF + c * FFN_CHUNK)):
            cols = slice(off, off + FFN_CHUNK)
            bcols = slice(part * FFN_CHUNK, (part + 1) * FFN_CHUNK)
            buf[HEAD:HEAD + SUB, bcols] = _dot(hn_buf[rows_of(u), :], w_up_ref[:, cols])
            for j in range(HEAD // SUBLANES):
                vrows = slice(j * SUBLANES, (j + 1) * SUBLANES)
                cur = buf[SUB + j * SUBLANES:SUB + (j + 1) * SUBLANES, bcols]
                buf[vrows, bcols] = jnp.where(first_sub, pltpu.roll(ffn_halo[vrows, cols], 1, axis=0),
                                              pltpu.roll(cur, 1, axis=0))
                ffn_halo[vrows, cols] = cur
            conved.append(ffncw_ref[0:1, cols] * buf[0:SUB, bcols]
                          + ffncw_ref[1:2, cols] * buf[SUBLANES:SUBLANES + SUB, bcols]
                          + ffncw_ref[2:3, cols] * buf[HEAD:HEAD + SUB, bcols])
        gate, val = conved
        act_buf[rows_of(u), c * FFN_CHUNK:(c + 1) * FFN_CHUNK] = (jax.nn.silu(gate) * val).astype(BF16)

    def down(u):
        d = _dot(act_buf[rows_of(u), :], w_down_ref[...])
        for c in range(n_slabs):
            perm_buf[u * n_slabs + c] = d[:, c * LANES:(c + 1) * LANES]

    def finish(u):
        groups = []
        for a in range(SUB // SUBLANES):
            src = _interleaved_rows(a)
            groups.append(jnp.concatenate([perm_buf[u * n_slabs + c, src, :] for c in range(n_slabs)], axis=1))
        hout = o_ref[rows_of(u), :] + jnp.concatenate(groups, axis=0)
        o_ref[rows_of(u), :] = _rms(hout, fing_ref[...])

    for phase in (norm_in, qkv, attention, short_conv, merge, norm_mid):
        for u in range(N_SUB):
            phase(u)
    for c in range(D_FF // FFN_CHUNK):
        for u in range(N_SUB):
            ffn_chunk(u, c)
    for phase in (down, finish):
        for u in range(N_SUB):
            phase(u)


def _resident(shape):
    return pl.BlockSpec(shape, lambda b, s: (0,) * len(shape), pipeline_mode=pl.Buffered(1))


@jax.jit
def kernel(x, mix_norm, w_in, b_in, sinks, conv_w, w_attn_branch, w_conv_branch, w_out, ffn_norm, w_up, ffn_conv_w,
           w_down, final_norm):
    batch, seq, d_model = x.shape
    assert d_model == D_MODEL and seq % TILE == 0 and mix_norm.shape[0] == 1
    T = TILE
    tile_spec = pl.BlockSpec((None, T, D_MODEL), lambda b, s: (b, s, 0))
    in_specs = [
        pl.BlockSpec(memory_space=pltpu.SMEM),
        tile_spec,
        _resident((1, D_MODEL)),
        pl.BlockSpec(memory_space=pl.ANY),
        _resident((1, IN_WIDTH)),
        _resident((CONV_K, CONV_WIDTH)),
        pl.BlockSpec(memory_space=pl.ANY),
        pl.BlockSpec(memory_space=pl.ANY),
        pl.BlockSpec(memory_space=pl.ANY),
        _resident((1, D_MODEL)),
        pl.BlockSpec(memory_space=pl.ANY),
        _resident((CONV_K, 2 * D_FF)),
        pl.BlockSpec(memory_space=pl.ANY),
        _resident((1, D_MODEL)),
    ]
    scratch = [
        pltpu.VMEM((D_MODEL, IN_WIDTH), BF16),
        pltpu.VMEM((ATTN_WIDTH, D_MODEL), BF16),
        pltpu.VMEM((CONV_WIDTH, D_MODEL), BF16),
        pltpu.VMEM((D_MODEL, D_MODEL), BF16),
        pltpu.VMEM((D_MODEL, 2 * D_FF), BF16),
        pltpu.VMEM((D_FF, D_MODEL), BF16),
        pltpu.VMEM((WIDE_SLOTS, WIDE_CHUNK_ROWS, 2 * D_FF), F32),
        pltpu.VMEM((TALL_SLOTS, TALL_CHUNK_ROWS, D_MODEL), F32),
        pltpu.SemaphoreType.DMA((WIDE_SLOTS,)),
        pltpu.SemaphoreType.DMA((TALL_SLOTS,)),
        pltpu.VMEM((T, D_MODEL), BF16),
        pltpu.VMEM((T, ATTN_WIDTH), BF16),
        pltpu.VMEM((4, T + BLOCK, LANES), BF16),
        pltpu.VMEM((4, T + BLOCK, 2 * LANES), BF16),
        pltpu.VMEM((T, ATTN_WIDTH), BF16),
        pltpu.VMEM((T + HALO, CONV_WIDTH), F32),
        pltpu.VMEM((T, CONV_WIDTH), BF16),
        pltpu.VMEM((T, D_MODEL), BF16),
        pltpu.VMEM((N_SUB * D_MODEL // LANES, SUB, LANES), F32),
        pltpu.VMEM((N_SUB, 2, SUB + HEAD, 2 * FFN_CHUNK), F32),
        pltpu.VMEM((HEAD, 2 * D_FF), F32),
        pltpu.VMEM((T, D_FF), BF16),
    ]
    call = pl.pallas_call(
        _layer_kernel,
        grid=(batch, seq // T),
        in_specs=in_specs,
        out_specs=tile_spec,
        out_shape=jax.ShapeDtypeStruct(x.shape, F32),
        scratch_shapes=scratch,
        compiler_params=pltpu.CompilerParams(
            dimension_semantics=("arbitrary", "arbitrary"),
            vmem_limit_bytes=VMEM_LIMIT_BYTES),
        name="hybrid_layer",
    )
    return call(
        sinks[0], x, mix_norm, w_in[0], b_in, conv_w[0], w_attn_branch[0], w_conv_branch[0], w_out[0],
        ffn_norm, w_up[0], ffn_conv_w[0], w_down[0], final_norm.reshape(1, D_MODEL))
```

```python
import jax
import jax.numpy as jnp
from jax import lax
from jax.experimental import pallas as pl
from jax.experimental.pallas import tpu as pltpu

D_MODEL = 1024
HEAD_DIM = 64
N_HEADS = 8
N_KV_HEADS = 2
GROUP = N_HEADS // N_KV_HEADS
WINDOW = 128
BLOCK = 128
ATTN_SCALE = HEAD_DIM ** -0.5
ATTN_WIDTH = N_HEADS * HEAD_DIM
KV_WIDTH = N_KV_HEADS * HEAD_DIM
CONV_WIDTH = 512
CONV_K = 3
IN_WIDTH = ATTN_WIDTH + 2 * KV_WIDTH + 3 * CONV_WIDTH + 2 * D_MODEL
D_FF = 2816
NORM_EPS = 1e-5
MASKED = -1e30

Q_OFF = 0
K_OFF = ATTN_WIDTH
V_OFF = K_OFF + KV_WIDTH
CB_OFF = V_OFF + KV_WIDTH
CC_OFF = CB_OFF + CONV_WIDTH
CX_OFF = CC_OFF + CONV_WIDTH
GA_OFF = CX_OFF + CONV_WIDTH
GC_OFF = GA_OFF + D_MODEL

LANES = 128
SUBLANES = 8
HALO = SUBLANES
TILE = 512
N_SUB = 2
SUB = TILE // N_SUB
RUN = SUB // SUBLANES
HEAD = (3 - 1) * SUBLANES
FFN_CHUNK = 256
ATTN_LOOKAHEAD = 3
WIDE_CHUNK_ROWS = 64
TALL_CHUNK_ROWS = 256
WIDE_SLOTS = 4
TALL_SLOTS = 3
VMEM_LIMIT_BYTES = 58 * 1024 * 1024

F32 = jnp.float32
BF16 = jnp.bfloat16


def _rms(x, g):
    ms = jnp.mean(x * x, axis=-1, keepdims=True)
    return x * lax.rsqrt(ms + NORM_EPS) * g


def _dot(a, b):
    return jnp.dot(a, b, preferred_element_type=F32)


def _dot_nt(a, b):
    return lax.dot_general(a, b, (((1,), (1,)), ((), ())), preferred_element_type=F32)


def _interleaved_rows(a):
    per_run = RUN // SUBLANES
    return pl.ds(SUBLANES * SUBLANES * (a % per_run) + a // per_run, SUBLANES, stride=SUBLANES)


class _WeightStream:
    def __init__(self, w_hbm, w_ref, stage, sem, chunk_rows):
        self.w_hbm, self.w_ref, self.stage, self.sem, self.chunk_rows = w_hbm, w_ref, stage, sem, chunk_rows
        self.cols = w_hbm.shape[1]
        self.n_chunks = w_hbm.shape[0] // chunk_rows
        self.n_slots = stage.shape[0]
        self.ahead = min(self.n_slots - 1, self.n_chunks)

    def _copy(self, i, slot):
        return pltpu.make_async_copy(self.w_hbm.at[pl.ds(i * self.chunk_rows, self.chunk_rows), :],
                                     self.stage.at[slot, :, pl.ds(0, self.cols)], self.sem.at[slot])

    def prime(self):
        for i in range(self.ahead):
            self._copy(i, i).start()

    def drain(self):
        def body(i, carry):
            @pl.when(i + self.ahead < self.n_chunks)
            def _():
                self._copy(i + self.ahead, (i + self.ahead) % self.n_slots).start()

            slot = i % self.n_slots
            self._copy(i, slot).wait()
            dst = pl.ds(pl.multiple_of(i * self.chunk_rows, self.chunk_rows), self.chunk_rows)
            self.w_ref[dst, :] = self.stage[slot, :, 0:self.cols].astype(BF16)
            return carry

        lax.fori_loop(0, self.n_chunks, body, 0)


def _layer_kernel(sinks_ref, x_ref, mixg_ref, w_in_hbm, b_in_ref, convw_ref, w_attn_hbm, w_conv_hbm, w_out_hbm,
                  ffng_ref, w_up_hbm, ffncw_ref, w_down_hbm, fing_ref, o_ref,
                  w_in_ref, w_attn_ref, w_conv_ref, w_out_ref, w_up_ref, w_down_ref, stage_wide, stage_tall, wide_sem,
                  tall_sem,
                  xn_buf, q_buf, k_ext, v_ext, attn_buf, u_buf, conv_buf, hn_buf, perm_buf, up_buf, ffn_halo,
                  act_buf):
    s = pl.program_id(1)
    T = TILE
    n_slabs = D_MODEL // LANES

    @pl.when((pl.program_id(0) == 0) & (s == 0))
    def _():
        w_in = _WeightStream(w_in_hbm, w_in_ref, stage_wide, wide_sem, WIDE_CHUNK_ROWS)
        w_up = _WeightStream(w_up_hbm, w_up_ref, stage_wide, wide_sem, WIDE_CHUNK_ROWS)
        w_attn = _WeightStream(w_attn_hbm, w_attn_ref, stage_tall, tall_sem, TALL_CHUNK_ROWS)
        w_conv = _WeightStream(w_conv_hbm, w_conv_ref, stage_tall, tall_sem, TALL_CHUNK_ROWS)
        w_out = _WeightStream(w_out_hbm, w_out_ref, stage_tall, tall_sem, TALL_CHUNK_ROWS)
        w_down = _WeightStream(w_down_hbm, w_down_ref, stage_tall, tall_sem, TALL_CHUNK_ROWS)
        w_in.prime()
        w_attn.prime()
        w_in.drain()
        w_up.prime()
        w_attn.drain()
        w_conv.prime()
        w_conv.drain()
        w_out.prime()
        w_out.drain()
        w_down.prime()
        w_up.drain()
        w_down.drain()

    @pl.when(s == 0)
    def _():
        k_ext[:, 0:BLOCK, :] = jnp.zeros((4, BLOCK, LANES), BF16)
        v_ext[:, 0:BLOCK, :] = jnp.zeros((4, BLOCK, 2 * LANES), BF16)
        u_buf[0:HALO, :] = jnp.zeros((HALO, CONV_WIDTH), F32)
        ffn_halo[...] = jnp.zeros((HEAD, 2 * D_FF), F32)

    @pl.when(s > 0)
    def _():
        for e in range(4):
            k_ext[e, 0:BLOCK, :] = k_ext[e, T:T + BLOCK, :]
            v_ext[e, 0:BLOCK, :] = v_ext[e, T:T + BLOCK, :]
        u_buf[0:HALO, :] = u_buf[T:T + HALO, :]

    def rows_of(u):
        return slice(u * SUB, (u + 1) * SUB)

    def norm_in(u):
        xn_buf[rows_of(u), :] = _rms(x_ref[rows_of(u), :], mixg_ref[...]).astype(BF16)

    def proj(u, lo, width):
        return _dot(xn_buf[rows_of(u), :], w_in_ref[:, lo:lo + width]) + b_in_ref[:, lo:lo + width]

    lane = lax.broadcasted_iota(jnp.int32, (SUB, LANES), 1)
    lo_half = lane < HEAD_DIM
    zero = jnp.zeros((SUB, LANES), F32)
    one = jnp.ones((SUB, LANES), F32)
    ones_lo = jnp.where(lo_half, one, zero).astype(BF16)
    ones_hi = jnp.where(lo_half, zero, one).astype(BF16)

    def qkv(u):
        q_buf[rows_of(u), :] = (proj(u, Q_OFF, ATTN_WIDTH) * ATTN_SCALE).astype(BF16)
        kv = proj(u, K_OFF, 2 * KV_WIDTH)
        k = kv[:, :KV_WIDTH]
        v = kv[:, KV_WIDTH:]
        k_sw = pltpu.roll(k, HEAD_DIM, axis=1)
        v_sw = pltpu.roll(v, HEAD_DIM, axis=1)
        rows = slice(BLOCK + u * SUB, BLOCK + (u + 1) * SUB)
        for h, (t_lo, t_hi) in enumerate(((k, k_sw), (k_sw, k))):
            k_ext[2 * h, rows, :] = jnp.where(lo_half, t_lo, zero).astype(BF16)
            k_ext[2 * h + 1, rows, :] = jnp.where(lo_half, zero, t_hi).astype(BF16)
        for h, (t_lo, t_hi) in enumerate(((v, v_sw), (v_sw, v))):
            v_ext[2 * h, rows, 0:LANES] = jnp.where(lo_half, t_lo, zero).astype(BF16)
            v_ext[2 * h, rows, LANES:] = ones_lo
            v_ext[2 * h + 1, rows, 0:LANES] = jnp.where(lo_half, zero, t_hi).astype(BF16)
            v_ext[2 * h + 1, rows, LANES:] = ones_hi

    q_pos = lax.broadcasted_iota(jnp.int32, (2 * BLOCK, BLOCK), 0) % BLOCK
    q_onehot = (lax.broadcasted_iota(jnp.int32, (2 * BLOCK, BLOCK), 1) == q_pos).astype(BF16)
    kj = lax.broadcasted_iota(jnp.int32, (2 * BLOCK, BLOCK), 0)
    dist = lax.broadcasted_iota(jnp.int32, (2 * BLOCK, BLOCK), 1) + BLOCK - kj
    in_band = (dist >= 0) & (dist < WINDOW)
    first_key = jnp.where(s == 0, BLOCK, 0)
    key_bias = jnp.where(in_band, 0.0, MASKED).astype(BF16)
    key_bias_first = jnp.where(in_band & (kj >= first_key), 0.0, MASKED).astype(BF16)
    top_rows = lax.broadcasted_iota(jnp.int32, (2 * BLOCK, 1), 0) < BLOCK
    lo_half_o = lax.broadcasted_iota(jnp.int32, (2 * BLOCK, LANES), 1) < HEAD_DIM

    def attn_scores(j, h):
        bias = key_bias_first if j == 0 else key_bias
        qrows = slice(j * BLOCK, (j + 1) * BLOCK)
        krows = slice(j * BLOCK, (j + 2) * BLOCK)
        qq = jnp.concatenate([q_buf[qrows, (2 * h) * LANES:(2 * h + 1) * LANES],
                              q_buf[qrows, (2 * h + 1) * LANES:(2 * h + 2) * LANES]], axis=0)
        qq = jnp.concatenate([qq, q_onehot], axis=1)
        keys = jnp.concatenate([jnp.concatenate([k_ext[2 * h + half, krows, :], bias], axis=1)
                                for half in range(2)], axis=0)
        return _dot_nt(qq, keys)

    def attn_values(j, h, sc):
        qrows = slice(j * BLOCK, (j + 1) * BLOCK)
        krows = slice(j * BLOCK, (j + 2) * BLOCK)
        probs, esink = [], []
        for half in range(2):
            sch = sc[:, half * 2 * BLOCK:(half + 1) * 2 * BLOCK]
            sink = jnp.where(top_rows, sinks_ref[GROUP * h + half], sinks_ref[GROUP * h + 2 + half])
            m = jnp.maximum(jnp.max(sch, axis=-1, keepdims=True), sink)
            probs.append(jnp.exp(sch - m).astype(BF16))
            esink.append(jnp.exp(sink - m))
        vals = jnp.concatenate([v_ext[2 * h + half, krows, :] for half in range(2)], axis=0)
        acc = _dot(jnp.concatenate(probs, axis=1), vals)
        den = acc[:, LANES:] + jnp.where(lo_half_o, esink[0], esink[1])
        o = (acc[:, :LANES] / den).astype(BF16)
        attn_buf[qrows, (2 * h) * LANES:(2 * h + 1) * LANES] = o[:BLOCK]
        attn_buf[qrows, (2 * h + 1) * LANES:(2 * h + 2) * LANES] = o[BLOCK:]

    def attention(u):
        units = [(j, h) for j in range(u * SUB // BLOCK, (u + 1) * SUB // BLOCK) for h in range(N_KV_HEADS)]
        scores = {}
        for unit in units[:ATTN_LOOKAHEAD]:
            scores[unit] = attn_scores(*unit)
        for i, unit in enumerate(units):
            if i + ATTN_LOOKAHEAD < len(units):
                nxt = units[i + ATTN_LOOKAHEAD]
                scores[nxt] = attn_scores(*nxt)
            attn_values(*unit, scores.pop(unit))

    def short_conv(u):
        lo = HALO + u * SUB
        uu = proj(u, CC_OFF, CONV_WIDTH) * proj(u, CX_OFF, CONV_WIDTH)
        u_buf[lo:lo + SUB, :] = uu
        conv = (convw_ref[0:1, :] * u_buf[lo - 2:lo - 2 + SUB, :]
                + convw_ref[1:2, :] * u_buf[lo - 1:lo - 1 + SUB, :]
                + convw_ref[2:3, :] * uu)
        conv_buf[rows_of(u), :] = (proj(u, CB_OFF, CONV_WIDTH) * conv).astype(BF16)

    def merge(u):
        r = rows_of(u)
        merged = (jax.nn.sigmoid(proj(u, GA_OFF, D_MODEL)) * _dot(attn_buf[r, :], w_attn_ref[...])
                  + jax.nn.sigmoid(proj(u, GC_OFF, D_MODEL)) * _dot(conv_buf[r, :], w_conv_ref[...]))
        o_ref[r, :] = x_ref[r, :] + _dot(merged.astype(BF16), w_out_ref[...])

    def norm_mid(u):
        hn = _rms(o_ref[rows_of(u), :], ffng_ref[...])
        for a in range(SUB // SUBLANES):
            dst = _interleaved_rows(a)
            for c in range(n_slabs):
                perm_buf[u * n_slabs + c, dst, :] = hn[a * SUBLANES:(a + 1) * SUBLANES, c * LANES:(c + 1) * LANES]
        hn_buf[rows_of(u), :] = jnp.concatenate(
            [perm_buf[u * n_slabs + c] for c in range(n_slabs)], axis=1).astype(BF16)

    first_sub = lax.broadcasted_iota(jnp.int32, (SUBLANES, FFN_CHUNK), 0) == 0

    def ffn_chunk(u, c):
        buf = up_buf.at[u, c % 2]
        conved = []
        for part, off in enumerate((c * FFN_CHUNK, D_FF + c * FFN_CHUNK)):
            cols = slice(off, off + FFN_CHUNK)
            bcols = slice(part * FFN_CHUNK, (part + 1) * FFN_CHUNK)
            buf[HEAD:HEAD + SUB, bcols] = _dot(hn_buf[rows_of(u), :], w_up_ref[:, cols])
            for j in range(HEAD // SUBLANES):
                vrows = slice(j * SUBLANES, (j + 1) * SUBLANES)
                cur = buf[SUB + j * SUBLANES:SUB + (j + 1) * SUBLANES, bcols]
                buf[vrows, bcols] = jnp.where(first_sub, pltpu.roll(ffn_halo[vrows, cols], 1, axis=0),
                                              pltpu.roll(cur, 1, axis=0))
                ffn_halo[vrows, cols] = cur
            conved.append(ffncw_ref[0:1, cols] * buf[0:SUB, bcols]
                          + ffncw_ref[1:2, cols] * buf[SUBLANES:SUBLANES + SUB, bcols]
                          + ffncw_ref[2:3, cols] * buf[HEAD:HEAD + SUB, bcols])
        gate, val = conved
        act_buf[rows_of(u), c * FFN_CHUNK:(c + 1) * FFN_CHUNK] = (jax.nn.silu(gate) * val).astype(BF16)

    def down(u):
        d = _dot(act_buf[rows_of(u), :], w_down_ref[...])
        for c in range(n_slabs):
            perm_buf[u * n_slabs + c] = d[:, c * LANES:(c + 1) * LANES]

    def finish(u):
        groups = []
        for a in range(SUB // SUBLANES):
            src = _interleaved_rows(a)
            groups.append(jnp.concatenate([perm_buf[u * n_slabs + c, src, :] for c in range(n_slabs)], axis=1))
        hout = o_ref[rows_of(u), :] + jnp.concatenate(groups, axis=0)
        o_ref[rows_of(u), :] = _rms(hout, fing_ref[...])

    for phase in (norm_in, qkv, attention, short_conv, merge, norm_mid):
        for u in range(N_SUB):
            phase(u)
    for c in range(D_FF // FFN_CHUNK):
        for u in range(N_SUB):
            ffn_chunk(u, c)
    for phase in (down, finish):
        for u in range(N_SUB):
            phase(u)


def _resident(shape):
    return pl.BlockSpec(shape, lambda b, s: (0,) * len(shape), pipeline_mode=pl.Buffered(1))


@jax.jit
def kernel(x, mix_norm, w_in, b_in, sinks, conv_w, w_attn_branch, w_conv_branch, w_out, ffn_norm, w_up, ffn_conv_w,
           w_down, final_norm):
    batch, seq, d_model = x.shape
    assert d_model == D_MODEL and seq % TILE == 0 and mix_norm.shape[0] == 1
    T = TILE
    tile_spec = pl.BlockSpec((None, T, D_MODEL), lambda b, s: (b, s, 0))
    in_specs = [
        pl.BlockSpec(memory_space=pltpu.SMEM),
        tile_spec,
        _resident((1, D_MODEL)),
        pl.BlockSpec(memory_space=pl.ANY),
        _resident((1, IN_WIDTH)),
        _resident((CONV_K, CONV_WIDTH)),
        pl.BlockSpec(memory_space=pl.ANY),
        pl.BlockSpec(memory_space=pl.ANY),
        pl.BlockSpec(memory_space=pl.ANY),
        _resident((1, D_MODEL)),
        pl.BlockSpec(memory_space=pl.ANY),
        _resident((CONV_K, 2 * D_FF)),
        pl.BlockSpec(memory_space=pl.ANY),
        _resident((1, D_MODEL)),
    ]
    scratch = [
        pltpu.VMEM((D_MODEL, IN_WIDTH), BF16),
        pltpu.VMEM((ATTN_WIDTH, D_MODEL), BF16),
        pltpu.VMEM((CONV_WIDTH, D_MODEL), BF16),
        pltpu.VMEM((D_MODEL, D_MODEL), BF16),
        pltpu.VMEM((D_MODEL, 2 * D_FF), BF16),
        pltpu.VMEM((D_FF, D_MODEL), BF16),
        pltpu.VMEM((WIDE_SLOTS, WIDE_CHUNK_ROWS, 2 * D_FF), F32),
        pltpu.VMEM((TALL_SLOTS, TALL_CHUNK_ROWS, D_MODEL), F32),
        pltpu.SemaphoreType.DMA((WIDE_SLOTS,)),
        pltpu.SemaphoreType.DMA((TALL_SLOTS,)),
        pltpu.VMEM((T, D_MODEL), BF16),
        pltpu.VMEM((T, ATTN_WIDTH), BF16),
        pltpu.VMEM((4, T + BLOCK, LANES), BF16),
        pltpu.VMEM((4, T + BLOCK, 2 * LANES), BF16),
        pltpu.VMEM((T, ATTN_WIDTH), BF16),
        pltpu.VMEM((T + HALO, CONV_WIDTH), F32),
        pltpu.VMEM((T, CONV_WIDTH), BF16),
        pltpu.VMEM((T, D_MODEL), BF16),
        pltpu.VMEM((N_SUB * D_MODEL // LANES, SUB, LANES), F32),
        pltpu.VMEM((N_SUB, 2, SUB + HEAD, 2 * FFN_CHUNK), F32),
        pltpu.VMEM((HEAD, 2 * D_FF), F32),
        pltpu.VMEM((T, D_FF), BF16),
    ]
    call = pl.pallas_call(
        _layer_kernel,
        grid=(batch, seq // T),
        in_specs=in_specs,
        out_specs=tile_spec,
        out_shape=jax.ShapeDtypeStruct(x.shape, F32),
        scratch_shapes=scratch,
        compiler_params=pltpu.CompilerParams(
            dimension_semantics=("arbitrary", "arbitrary"),
            vmem_limit_bytes=VMEM_LIMIT_BYTES),
        name="hybrid_layer",
    )
    return call(
        sinks[0], x, mix_norm, w_in[0], b_in, conv_w[0], w_attn_branch[0], w_conv_branch[0], w_out[0],
        ffn_norm, w_up[0], ffn_conv_w[0], w_down[0], final_norm.reshape(1, D_MODEL))
```
